```python
import jax, jax.numpy as jnp
from jax import lax
import numpy as np

D_MODEL = 1024
BATCH = 16
SEQ = 4096
DEPTH = 2

N_META = 16
A_HEADS = 4
A_QK_DIM = 128
A_V_DIM = 256
A_CHUNK = 64
GATE_PAD = -1e30
B_HEADS = 8
B_HEAD_DIM = 128
B_BLOCK = 128
D_FF_DENSE = 2816
N_EXPERTS = 8
TOP_K = 2
D_FF_EXPERT = 3584
MOE_BLOCK = 256
N_A_LAYERS = DEPTH // 2
N_B_LAYERS = DEPTH - N_A_LAYERS
N_DENSE_LAYERS = (DEPTH + 1) // 2
N_MOE_LAYERS = DEPTH // 2
DEEPNORM_ALPHA = (2 * DEPTH) ** 0.25
DEEPNORM_BETA = (8 * DEPTH) ** -0.25
LN_EPS = 1e-5

kernel_name = "yoco_mlstm_stickbreak_moe_deepnorm"


def _layernorm(x, g, b):
    xf = x.astype(jnp.float32)
    mu = xf.mean(-1, keepdims=True)
    var = jnp.mean(jnp.square(xf - mu), -1, keepdims=True)
    return ((xf - mu) * lax.rsqrt(var + LN_EPS) * g.astype(jnp.float32) + b.astype(jnp.float32)).astype(x.dtype)


def _mlstm_chunk(carry, xs):
    c_mat, n_vec, m = carry
    q, k, v, li, lf = xs
    b = jnp.cumsum(lf, axis=-1)
    causal = jnp.tril(jnp.ones((A_CHUNK, A_CHUNK), dtype=bool))
    d_log = jnp.where(causal, b[..., :, None] - b[..., None, :] + li[..., None, :], -jnp.inf)
    inter = b + m[..., None]
    m_t = jnp.maximum(inter, d_log.max(-1))
    w_intra = jnp.exp(d_log - m_t[..., None])
    w_inter = jnp.exp(inter - m_t)
    s = jnp.einsum('bhtd,bhsd->bhts', q, k) * w_intra
    num = w_inter[..., None] * jnp.einsum('bhtd,bhde->bhte', q, c_mat) + jnp.einsum('bhts,bhse->bhte', s, v)
    den = w_inter * jnp.einsum('bhtd,bhd->bht', q, n_vec) + s.sum(-1)
    h = num / jnp.maximum(jnp.abs(den), jnp.exp(-m_t))[..., None]
    b_last = b[..., -1]
    g = b_last[..., None] - b + li
    m_new = jnp.maximum(b_last + m, g.max(-1))
    decay = jnp.exp(b_last + m - m_new)
    wk = jnp.exp(g - m_new[..., None])[..., None] * k
    c_new = decay[..., None, None] * c_mat + jnp.einsum('bhsd,bhse->bhde', wk, v)
    n_new = decay[..., None] * n_vec + wk.sum(2)
    return (c_new, n_new, m_new), h


def _mlstm_mixer(h, w_in, b_gate, norm_g, w_out):
    bsz, length, _ = h.shape
    hk, hv = A_HEADS * A_QK_DIM, A_HEADS * A_V_DIM
    proj = h @ w_in
    q = proj[..., :hk]
    k = proj[..., hk:2 * hk]
    v = proj[..., 2 * hk:2 * hk + hv]
    o = proj[..., 2 * hk + hv:2 * hk + 2 * hv]
    gates = (proj[..., 2 * hk + 2 * hv:] + b_gate).astype(jnp.float32)
    li = gates[..., :A_HEADS].transpose(0, 2, 1)
    lf = jax.nn.log_sigmoid(gates[..., A_HEADS:]).transpose(0, 2, 1)

    def heads(t, dh):
        return t.reshape(bsz, length, A_HEADS, dh).transpose(0, 2, 1, 3).astype(jnp.float32)

    qh = heads(q, A_QK_DIM)
    kh = heads(k, A_QK_DIM) * (A_QK_DIM ** -0.5)
    vh = heads(v, A_V_DIM)
    pad = A_CHUNK - N_META

    def padseq(t, val):
        return jnp.pad(t, [(0, 0), (0, 0), (pad, 0)] + [(0, 0)] * (t.ndim - 3), constant_values=val)

    n_chunks = (length + pad) // A_CHUNK

    def chunks(t):
        return jnp.moveaxis(t.reshape((bsz, A_HEADS, n_chunks, A_CHUNK) + t.shape[3:]), 2, 0)

    xs = (chunks(padseq(qh, 0.0)), chunks(padseq(kh, 0.0)), chunks(padseq(vh, 0.0)),
          chunks(padseq(li, GATE_PAD)), chunks(padseq(lf, 0.0)))
    init = (jnp.zeros((bsz, A_HEADS, A_QK_DIM, A_V_DIM), jnp.float32),
            jnp.zeros((bsz, A_HEADS, A_QK_DIM), jnp.float32),
            jnp.zeros((bsz, A_HEADS), jnp.float32))
    _, hs = lax.scan(_mlstm_chunk, init, xs)
    hs = jnp.moveaxis(hs, 0, 2).reshape(bsz, A_HEADS, n_chunks * A_CHUNK, A_V_DIM)[:, :, pad:]
    hs = hs.transpose(0, 2, 1, 3)
    mu = hs.mean(-1, keepdims=True)
    var = jnp.mean(jnp.square(hs - mu), -1, keepdims=True)
    hn = (hs - mu) * lax.rsqrt(var + LN_EPS) * norm_g.astype(jnp.float32).reshape(A_HEADS, A_V_DIM)
    out = jax.nn.sigmoid(o.astype(jnp.float32)) * hn.reshape(bsz, length, hv)
    return out.astype(h.dtype) @ w_out


def _shared_kv(h, w_kv):
    bsz, length, _ = h.shape
    kv = (h @ w_kv).reshape(bsz, length, 2, B_HEADS, B_HEAD_DIM).transpose(2, 0, 3, 1, 4).astype(jnp.float32)
    kv = jnp.pad(kv, [(0, 0), (0, 0), (0, 0), (B_BLOCK - N_META, 0), (0, 0)])
    return kv[0], kv[1]


def _stick_breaking_mixer(h, w_q, k_sh, v_sh, w_o):
    bsz, length, _ = h.shape
    pad = B_BLOCK - N_META
    q = (h @ w_q).reshape(bsz, length, B_HEADS, B_HEAD_DIM).transpose(0, 2, 1, 3).astype(jnp.float32)
    q = jnp.pad(q, [(0, 0), (0, 0), (pad, 0), (0, 0)]) * (B_HEAD_DIM ** -0.5)
    total = length + pad
    n_blk = total // B_BLOCK
    qb = jnp.moveaxis(q.reshape(bsz, B_HEADS, n_blk, B_BLOCK, B_HEAD_DIM), 2, 0)
    kpos = jnp.arange(total)

    def block(args):
        q_blk, blk = args
        qpos = blk * B_BLOCK + jnp.arange(B_BLOCK)
        z = jnp.einsum('bhqd,bhkd->bhqk', q_blk, k_sh)
        valid = (kpos[None, :] < qpos[:, None]) & (kpos[None, :] >= pad)
        sp = jnp.where(valid, jax.nn.softplus(z), 0.0)
        suffix = lax.cumsum(sp, axis=3, reverse=True) - sp
        a = jnp.where(valid, jnp.exp(jax.nn.log_sigmoid(z) - suffix), 0.0)
        return jnp.einsum('bhqk,bhkd->bhqd', a, v_sh)

    ob = lax.map(block, (qb, jnp.arange(n_blk)))
    o = jnp.moveaxis(ob, 0, 2).reshape(bsz, B_HEADS, total, B_HEAD_DIM)[:, :, pad:]
    o = o.transpose(0, 2, 1, 3).reshape(bsz, length, B_HEADS * B_HEAD_DIM).astype(h.dtype)
    return o @ w_o


def _swiglu(x, w_gate, w_up, w_down):
    return (jax.nn.silu(x @ w_gate) * (x @ w_up)) @ w_down


def _moe_swiglu(x2, w_router, w_gate, w_up, w_down):
    n = x2.shape[0]
    logits = x2.astype(jnp.float32) @ w_router.astype(jnp.float32)
    top_val, top_idx = lax.top_k(logits, TOP_K)
    gates = jax.nn.softmax(top_val, axis=-1)
    flat_e = top_idx.reshape(-1)
    flat_tok = jnp.repeat(jnp.arange(n, dtype=jnp.int32), TOP_K)
    flat_g = gates.reshape(-1)
    order = jnp.argsort(flat_e)
    se, stok, sg = flat_e[order], flat_tok[order], flat_g[order]
    counts = jnp.bincount(flat_e, length=N_EXPERTS)
    start = jnp.cumsum(counts) - counts
    pcounts = (counts + MOE_BLOCK - 1) // MOE_BLOCK * MOE_BLOCK
    pend = jnp.cumsum(pcounts)
    pstart = pend - pcounts
    dest = pstart[se] + (jnp.arange(n * TOP_K) - start[se])
    n_blk = -(-(n * TOP_K + N_EXPERTS * (MOE_BLOCK - 1)) // MOE_BLOCK)
    rows = n_blk * MOE_BLOCK
    tok_buf = jnp.full((rows,), n, dtype=jnp.int32).at[dest].set(stok)
    x_pad = jnp.concatenate([x2, jnp.zeros((1, x2.shape[1]), x2.dtype)], axis=0)
    xb = x_pad[tok_buf].reshape(n_blk, MOE_BLOCK, x2.shape[1])
    blk_e = jnp.minimum(jnp.searchsorted(pend, jnp.arange(n_blk) * MOE_BLOCK, side='right'), N_EXPERTS - 1)

    def expert_block(args):
        x_blk, e = args
        return _swiglu(x_blk, w_gate[e], w_up[e], w_down[e])

    yb = lax.map(expert_block, (xb, blk_e)).reshape(rows, x2.shape[1])
    y = yb[dest] * sg[:, None].astype(yb.dtype)
    return jnp.zeros_like(x2).at[stok].add(y)


def setup_inputs(seed: int = 0) -> dict:
    key = jax.random.key(seed)
    ks = jax.random.split(key, 20)
    f32 = jnp.float32
    hk, hv = A_HEADS * A_QK_DIM, A_HEADS * A_V_DIM
    a_in = 2 * hk + 2 * hv + 2 * A_HEADS
    bw = B_HEADS * B_HEAD_DIM

    def nrm(k, shape, fan_in, scale=1.0):
        return jax.random.normal(k, shape, f32) * (fan_in ** -0.5) * scale

    f_bias = jnp.linspace(3.0, 6.0, A_HEADS, dtype=f32)
    b_gate = jnp.concatenate([jnp.zeros((N_A_LAYERS, A_HEADS), f32),
                              jnp.broadcast_to(f_bias, (N_A_LAYERS, A_HEADS))], axis=-1)
    b_gate = b_gate + 0.1 * jax.random.normal(ks[3], (N_A_LAYERS, 2 * A_HEADS), f32)
    return {
        "x": jax.random.normal(ks[0], (BATCH, SEQ, D_MODEL), f32),
        "meta": jax.random.normal(ks[1], (N_META, D_MODEL), f32),
        "w_in_a": nrm(ks[2], (N_A_LAYERS, D_MODEL, a_in), D_MODEL),
        "b_gate_a": b_gate,
        "norm_a": 1.0 + 0.02 * jax.random.normal(ks[4], (N_A_LAYERS, hv), f32),
        "w_out_a": nrm(ks[5], (N_A_LAYERS, hv, D_MODEL), hv, DEEPNORM_BETA),
        "w_kv": nrm(ks[6], (D_MODEL, 2 * bw), D_MODEL),
        "w_q_b": nrm(ks[7], (N_B_LAYERS, D_MODEL, bw), D_MODEL),
        "w_o_b": nrm(ks[8], (N_B_LAYERS, bw, D_MODEL), bw, DEEPNORM_BETA),
        "w_gate_d": nrm(ks[9], (N_DENSE_LAYERS, D_MODEL, D_FF_DENSE), D_MODEL),
        "w_up_d": nrm(ks[10], (N_DENSE_LAYERS, D_MODEL, D_FF_DENSE), D_MODEL),
        "w_down_d": nrm(ks[11], (N_DENSE_LAYERS, D_FF_DENSE, D_MODEL), D_FF_DENSE, DEEPNORM_BETA),
        "w_router": nrm(ks[12], (N_MOE_LAYERS, D_MODEL, N_EXPERTS), D_MODEL),
        "w_gate_e": nrm(ks[13], (N_MOE_LAYERS, N_EXPERTS, D_MODEL, D_FF_EXPERT), D_MODEL),
        "w_up_e": nrm(ks[14], (N_MOE_LAYERS, N_EXPERTS, D_MODEL, D_FF_EXPERT), D_MODEL),
        "w_down_e": nrm(ks[15], (N_MOE_LAYERS, N_EXPERTS, D_FF_EXPERT, D_MODEL), D_FF_EXPERT, DEEPNORM_BETA),
        "ln_g": 1.0 + 0.02 * jax.random.normal(ks[16], (DEPTH, 2, D_MODEL), f32),
        "ln_b": 0.02 * jax.random.normal(ks[17], (DEPTH, 2, D_MODEL), f32),
    }


def reference(x, meta, w_in_a, b_gate_a, norm_a, w_out_a, w_kv, w_q_b, w_o_b,
              w_gate_d, w_up_d, w_down_d, w_router, w_gate_e, w_up_e, w_down_e, ln_g, ln_b):
    bsz = x.shape[0]
    h = jnp.concatenate([jnp.broadcast_to(meta[None].astype(x.dtype), (bsz, N_META, D_MODEL)), x], axis=1)
    k_sh = None
    v_sh = None
    for layer in range(DEPTH):
        if layer < N_A_LAYERS:
            mix = _mlstm_mixer(h, w_in_a[layer], b_gate_a[layer], norm_a[layer], w_out_a[layer])
        else:
            j = layer - N_A_LAYERS
            mix = _stick_breaking_mixer(h, w_q_b[j], k_sh, v_sh, w_o_b[j])
        h = _layernorm(DEEPNORM_ALPHA * h + mix, ln_g[layer, 0], ln_b[layer, 0])
        if layer % 2 == 0:
            i = layer // 2
            ffn = _swiglu(h, w_gate_d[i], w_up_d[i], w_down_d[i])
        else:
            i = layer // 2
            ffn = _moe_swiglu(h.reshape(-1, D_MODEL), w_router[i], w_gate_e[i], w_up_e[i], w_down_e[i]).reshape(h.shape)
        h = _layernorm(DEEPNORM_ALPHA * h + ffn, ln_g[layer, 1], ln_b[layer, 1])
        if layer == N_A_LAYERS - 1:
            k_sh, v_sh = _shared_kv(h, w_kv)
    return h[:, N_META:]
```

```python
import functools

import jax
import jax.numpy as jnp
from jax import lax
from jax.experimental import pallas as pl
from jax.experimental.pallas import tpu as pltpu

D_MODEL = 1024
N_META = 16
A_HEADS = 4
A_QK_DIM = 128
A_V_DIM = 256
GATE_PAD = -1e30
B_HEADS = 8
B_HEAD_DIM = 128
D_FF_DENSE = 2816
N_EXPERTS = 8
D_FF_EXPERT = 3584
DEPTH = 2
DEEPNORM_ALPHA = (2 * DEPTH) ** 0.25
LN_EPS = 1e-5

LANES = 128
META_BLOCK = 128
META_PAD = META_BLOCK - N_META
SEQ_CHUNK = 128
ROW_BLOCK = 512
FF_CHUNK = 256
ATT_BQ = 512
ATT_BK = 256
MOE_BM = 512
MOE_TF = 512
VMEM_LIMIT = 56 * 1024 * 1024

BF16 = jnp.bfloat16
F32 = jnp.float32


def _cparams(sem):
    return pltpu.CompilerParams(dimension_semantics=sem, vmem_limit_bytes=VMEM_LIMIT)


def _ln(t, g, b):
    mu = jnp.mean(t, axis=-1, keepdims=True)
    d = t - mu
    var = jnp.mean(d * d, axis=-1, keepdims=True)
    return d * lax.rsqrt(var + LN_EPS) * g + b


def _softplus(z):
    return jnp.maximum(z, 0.0) + jnp.log1p(jnp.exp(-jnp.abs(z)))


def _log_sigmoid(z):
    return jnp.minimum(z, 0.0) - jnp.log1p(jnp.exp(-jnp.abs(z)))


def _proj_kernel(*refs, scales):
    n = len(scales)
    x_ref, w_refs, o_refs = refs[0], refs[1:1 + n], refs[1 + n:]
    x = x_ref[...].astype(BF16)
    for w_ref, o_ref, s in zip(w_refs, o_refs, scales):
        y = jnp.dot(x, w_ref[...], preferred_element_type=F32)
        if s != 1.0:
            y = y * s
        o_ref[...] = y.astype(o_ref.dtype)


def _proj(x, ws, scales, dtypes, rows):
    k = x.shape[1]
    in_specs = [pl.BlockSpec((ROW_BLOCK, k), lambda i: (i, 0))]
    in_specs += [pl.BlockSpec(w.shape, lambda i: (0, 0)) for w in ws]
    out_specs = [pl.BlockSpec((ROW_BLOCK, w.shape[1]), lambda i: (i, 0)) for w in ws]
    out_shape = [jax.ShapeDtypeStruct((rows, w.shape[1]), dt) for w, dt in zip(ws, dtypes)]
    return pl.pallas_call(
        functools.partial(_proj_kernel, scales=tuple(scales)),
        grid=(rows // ROW_BLOCK,), in_specs=in_specs, out_specs=out_specs, out_shape=out_shape,
        compiler_params=_cparams(("parallel",)), name="proj")(x, *ws)


def _mm_res_ln_kernel(a_ref, w_ref, r_ref, g_ref, b_ref, o_ref):
    y = jnp.dot(a_ref[...], w_ref[...], preferred_element_type=F32)
    o_ref[...] = _ln(DEEPNORM_ALPHA * r_ref[...] + y, g_ref[...], b_ref[...])


def _mm_res_ln(a, w, res, g, b):
    rows, k = a.shape
    d = w.shape[1]
    return pl.pallas_call(
        _mm_res_ln_kernel, grid=(rows // ROW_BLOCK,),
        in_specs=[pl.BlockSpec((ROW_BLOCK, k), lambda i: (i, 0)),
                  pl.BlockSpec((k, d), lambda i: (0, 0)),
                  pl.BlockSpec((ROW_BLOCK, d), lambda i: (i, 0)),
                  pl.BlockSpec((1, d), lambda i: (0, 0)),
                  pl.BlockSpec((1, d), lambda i: (0, 0))],
        out_specs=pl.BlockSpec((ROW_BLOCK, d), lambda i: (i, 0)),
        out_shape=jax.ShapeDtypeStruct((rows, d), F32),
        compiler_params=_cparams(("parallel",)), name="mm_res_ln")(a, w, res, g, b)


def _mlstm_kernel(q_ref, k_ref, v_ref, o_ref, g_ref, gt_ref, bg_ref, bgt_ref, ng_ref, out_ref,
                  c_scr, n_scr, m_scr):
    c = pl.program_id(1)
    L = SEQ_CHUNK

    @pl.when(c == 0)
    def _():
        c_scr[...] = jnp.zeros_like(c_scr)
        n_scr[...] = jnp.zeros_like(n_scr)
        m_scr[...] = jnp.zeros_like(m_scr)

    t_idx = lax.broadcasted_iota(jnp.int32, (L, L), 0)
    s_idx = lax.broadcasted_iota(jnp.int32, (L, L), 1)
    causal = s_idx <= t_idx
    pad_col = jnp.logical_and(c == 0, lax.broadcasted_iota(jnp.int32, (L, 1), 0) < META_PAD)
    pad_row = jnp.logical_and(c == 0, lax.broadcasted_iota(jnp.int32, (1, L), 1) < META_PAD)
    gates = g_ref[:, 0:2 * A_HEADS] + bg_ref[...]
    gates_t = gt_ref[...] + bgt_ref[...]

    for h in range(A_HEADS):
        li_col = jnp.where(pad_col, GATE_PAD, gates[:, h:h + 1])
        lf_col = jnp.where(pad_col, 0.0, _log_sigmoid(gates[:, A_HEADS + h:A_HEADS + h + 1]))
        li_row = jnp.where(pad_row, GATE_PAD, gates_t[h:h + 1, :])
        lf_row = jnp.where(pad_row, 0.0, _log_sigmoid(gates_t[A_HEADS + h:A_HEADS + h + 1, :]))
        b_col = jnp.sum(jnp.where(causal, lf_row, 0.0), axis=1, keepdims=True)
        b_row = jnp.sum(jnp.where(t_idx <= s_idx, lf_col, 0.0), axis=0, keepdims=True)
        b_last = jnp.sum(lf_row, axis=1, keepdims=True)
        m_prev = m_scr[h][0:1, 0:1]

        d_log = jnp.where(causal, b_col - b_row + li_row, -jnp.inf)
        inter = b_col + m_prev
        m_t = jnp.maximum(inter, jnp.max(d_log, axis=1, keepdims=True))
        w_intra = jnp.exp(d_log - m_t)
        w_inter = jnp.exp(inter - m_t)

        qh = q_ref[:, h * A_QK_DIM:(h + 1) * A_QK_DIM]
        kh = k_ref[:, h * A_QK_DIM:(h + 1) * A_QK_DIM]
        vh = v_ref[:, h * A_V_DIM:(h + 1) * A_V_DIM]
        s = lax.dot_general(qh, kh, (((1,), (1,)), ((), ())), preferred_element_type=F32) * w_intra
        c_mat = c_scr[h]
        n_vec = n_scr[h][0:1, :]
        num = w_inter * jnp.dot(qh, c_mat.astype(BF16), preferred_element_type=F32) + \
            jnp.dot(s.astype(BF16), vh, preferred_element_type=F32)
        den = w_inter * jnp.sum(qh.astype(F32) * n_vec, axis=1, keepdims=True) + \
            jnp.sum(s, axis=1, keepdims=True)
        hh = num * (1.0 / jnp.maximum(jnp.abs(den), jnp.exp(-m_t)))

        g_col = b_last - b_col + li_col
        m_new = jnp.maximum(b_last + m_prev, jnp.max(g_col, axis=0, keepdims=True))
        decay = jnp.exp(b_last + m_prev - m_new)
        wk = jnp.exp(g_col - m_new) * kh.astype(F32)
        c_scr[h] = decay * c_mat + lax.dot_general(
            wk.astype(BF16), vh, (((0,), (0,)), ((), ())), preferred_element_type=F32)
        n_scr[h] = jnp.broadcast_to(decay * n_vec + jnp.sum(wk, axis=0, keepdims=True), n_scr.shape[1:])
        m_scr[h] = jnp.broadcast_to(m_new, m_scr.shape[1:])

        mu = jnp.mean(hh, axis=1, keepdims=True)
        dv = hh - mu
        var = jnp.mean(dv * dv, axis=1, keepdims=True)
        hn = dv * lax.rsqrt(var + LN_EPS) * ng_ref[:, h * A_V_DIM:(h + 1) * A_V_DIM]
        og = o_ref[:, h * A_V_DIM:(h + 1) * A_V_DIM].astype(F32)
        out_ref[:, h * A_V_DIM:(h + 1) * A_V_DIM] = (jax.nn.sigmoid(og) * hn).astype(out_ref.dtype)


def _mlstm(q, k, v, o, gates, gates_t, b_gate, norm_g, bsz, seq):
    rows = q.shape[0]
    n_chunks = 1 + seq // SEQ_CHUNK
    per_b = seq // SEQ_CHUNK
    meta0 = bsz * per_b

    def rb(b, c):
        return jnp.where(c == 0, meta0 + b, b * per_b + c - 1)

    hk, hv = A_HEADS * A_QK_DIM, A_HEADS * A_V_DIM
    return pl.pallas_call(
        _mlstm_kernel, grid=(bsz, n_chunks),
        in_specs=[pl.BlockSpec((SEQ_CHUNK, hk), lambda b, c: (rb(b, c), 0)),
                  pl.BlockSpec((SEQ_CHUNK, hk), lambda b, c: (rb(b, c), 0)),
                  pl.BlockSpec((SEQ_CHUNK, hv), lambda b, c: (rb(b, c), 0)),
                  pl.BlockSpec((SEQ_CHUNK, hv), lambda b, c: (rb(b, c), 0)),
                  pl.BlockSpec((SEQ_CHUNK, LANES), lambda b, c: (rb(b, c), 0)),
                  pl.BlockSpec((2 * A_HEADS, SEQ_CHUNK), lambda b, c: (0, rb(b, c))),
                  pl.BlockSpec((1, 2 * A_HEADS), lambda b, c: (0, 0)),
                  pl.BlockSpec((2 * A_HEADS, 1), lambda b, c: (0, 0)),
                  pl.BlockSpec((1, hv), lambda b, c: (0, 0))],
        out_specs=pl.BlockSpec((SEQ_CHUNK, hv), lambda b, c: (rb(b, c), 0)),
        out_shape=jax.ShapeDtypeStruct((rows, hv), BF16),
        scratch_shapes=[pltpu.VMEM((A_HEADS, A_QK_DIM, A_V_DIM), F32),
                        pltpu.VMEM((A_HEADS, 8, A_QK_DIM), F32),
                        pltpu.VMEM((A_HEADS, 8, LANES), F32)],
        compiler_params=_cparams(("parallel", "arbitrary")), name="mlstm")(
            q, k, v, o, gates, gates_t, b_gate.reshape(1, -1), b_gate.reshape(-1, 1), norm_g.reshape(1, -1))


def _ffn_kernel(x_ref, wg_ref, wu_ref, wd_ref, g_ref, b_ref, o_ref, xb_ref, acc_ref):
    xb_ref[...] = x_ref[...].astype(BF16)
    acc_ref[...] = jnp.zeros_like(acc_ref)

    def body(ci, carry):
        xb = xb_ref[...]
        gate = jnp.dot(xb, wg_ref[ci], preferred_element_type=F32)
        up = jnp.dot(xb, wu_ref[ci], preferred_element_type=F32)
        hid = (gate * jax.nn.sigmoid(gate) * up).astype(BF16)
        acc_ref[...] += jnp.dot(hid, wd_ref[ci], preferred_element_type=F32)
        return carry

    lax.fori_loop(0, wg_ref.shape[0], body, 0)
    o_ref[...] = _ln(DEEPNORM_ALPHA * x_ref[...] + acc_ref[...], g_ref[...], b_ref[...])


def _ffn(x, wg3, wu3, wd3, g, b):
    rows, d = x.shape
    whole = lambda a: pl.BlockSpec(a.shape, lambda i: (0,) * a.ndim, pipeline_mode=pl.Buffered(1))
    return pl.pallas_call(
        _ffn_kernel, grid=(rows // ROW_BLOCK,),
        in_specs=[pl.BlockSpec((ROW_BLOCK, d), lambda i: (i, 0)), whole(wg3), whole(wu3), whole(wd3),
                  pl.BlockSpec((1, d), lambda i: (0, 0)), pl.BlockSpec((1, d), lambda i: (0, 0))],
        out_specs=pl.BlockSpec((ROW_BLOCK, d), lambda i: (i, 0)),
        out_shape=jax.ShapeDtypeStruct((rows, d), F32),
        scratch_shapes=[pltpu.VMEM((ROW_BLOCK, d), BF16), pltpu.VMEM((ROW_BLOCK, d), F32)],
        compiler_params=_cparams(("parallel",)), name="ffn_dense")(x, wg3, wu3, wd3, g, b)


def _attn_kernel(q_ref, kr_ref, vr_ref, km_ref, vm_ref, o_ref):
    t = pl.program_id(2)
    bq, bk = ATT_BQ, ATT_BK
    n_kt = bq // bk
    q = q_ref[...]
    r_i = lax.broadcasted_iota(jnp.int32, (bk, bk), 0)
    c_i = lax.broadcasted_iota(jnp.int32, (bk, bk), 1)
    suffix_ones = (r_i >= c_i).astype(BF16)
    q_pos = t * bq + lax.broadcasted_iota(jnp.int32, (bq, 1), 0)

    def tile(kb, vb, valid, ones, carry):
        acc, run = carry
        z = lax.dot_general(q, kb, (((1,), (1,)), ((), ())), preferred_element_type=F32)
        sp = _softplus(z)
        if valid is not None:
            sp = jnp.where(valid, sp, 0.0)
        hi = sp.astype(BF16)
        lo = (sp - hi.astype(F32)).astype(BF16)
        suf = jnp.dot(hi, ones, preferred_element_type=F32) + jnp.dot(lo, ones, preferred_element_type=F32)
        a = jnp.exp(z - suf - run)
        if valid is not None:
            a = jnp.where(valid, a, 0.0)
        acc = acc + jnp.dot(a.astype(BF16), vb, preferred_element_type=F32)
        return acc, run + suf[:, 0:1]

    carry = (jnp.zeros((bq, B_HEAD_DIM), F32), jnp.zeros((bq, 1), F32))
    for d in range(n_kt):
        k0 = pl.multiple_of(t * bq + (n_kt - 1 - d) * bk, bk)
        k_pos = k0 + lax.broadcasted_iota(jnp.int32, (1, bk), 1)
        carry = tile(kr_ref[pl.ds(k0, bk), :], vr_ref[pl.ds(k0, bk), :], k_pos < q_pos, suffix_ones, carry)

    def body(i, carry):
        k0 = pl.multiple_of((t * n_kt - 1 - i) * bk, bk)
        return tile(kr_ref[pl.ds(k0, bk), :], vr_ref[pl.ds(k0, bk), :], None, suffix_ones, carry)

    carry = lax.fori_loop(0, t * n_kt, body, carry)
    meta_valid = lax.broadcasted_iota(jnp.int32, (1, META_BLOCK), 1) >= META_PAD
    acc, _ = tile(km_ref[...], vm_ref[...], meta_valid, suffix_ones[:META_BLOCK, :META_BLOCK], carry)
    o_ref[...] = acc.astype(o_ref.dtype)


def _attention(q, k, v, bsz, seq):
    n_qt = seq // ATT_BQ
    meta0 = bsz * seq // META_BLOCK
    hd = B_HEAD_DIM
    return pl.pallas_call(
        _attn_kernel, grid=(bsz, B_HEADS, n_qt),
        in_specs=[pl.BlockSpec((ATT_BQ, hd), lambda b, h, t: (b * n_qt + t, h)),
                  pl.BlockSpec((seq, hd), lambda b, h, t: (b, h)),
                  pl.BlockSpec((seq, hd), lambda b, h, t: (b, h)),
                  pl.BlockSpec((META_BLOCK, hd), lambda b, h, t: (meta0 + b, h)),
                  pl.BlockSpec((META_BLOCK, hd), lambda b, h, t: (meta0 + b, h))],
        out_specs=pl.BlockSpec((ATT_BQ, hd), lambda b, h, t: (b * n_qt + t, h)),
        out_shape=jax.ShapeDtypeStruct((bsz * seq, B_HEADS * hd), BF16),
        compiler_params=_cparams(("parallel", "parallel", "arbitrary")), name="stickbreak_attn")(q, k, v, k, v)


def _router_kernel(x_ref, w_ref, route_ref, cnt_ref, xb_ref, base_scr):
    i = pl.program_id(0)
    tb = x_ref.shape[0]

    @pl.when(i == 0)
    def _():
        base_scr[...] = jnp.zeros_like(base_scr)

    x = x_ref[...]
    xb_ref[...] = x.astype(BF16)
    logits = jnp.dot(x, w_ref[...], preferred_element_type=F32, precision=lax.Precision.HIGHEST)
    lane = lax.broadcasted_iota(jnp.int32, (tb, LANES), 1)
    logits = jnp.where(lane < N_EXPERTS, logits, -jnp.inf)
    v1 = jnp.max(logits, axis=1, keepdims=True)
    i1 = jnp.min(jnp.where(logits == v1, lane, LANES), axis=1, keepdims=True)
    rest = jnp.where(lane == i1, -jnp.inf, logits)
    v2 = jnp.max(rest, axis=1, keepdims=True)
    i2 = jnp.min(jnp.where(rest == v2, lane, LANES), axis=1, keepdims=True)
    e2 = jnp.exp(v2 - v1)
    g1 = 1.0 / (1.0 + e2)
    g2 = e2 / (1.0 + e2)
    sel1 = (lane == i1).astype(F32)
    sel2 = (lane == i2).astype(F32)
    onehot = sel1 + sel2
    r_i = lax.broadcasted_iota(jnp.int32, (tb, tb), 0)
    c_i = lax.broadcasted_iota(jnp.int32, (tb, tb), 1)
    before = (c_i < r_i).astype(BF16)
    prior = jnp.dot(before, onehot.astype(BF16), preferred_element_type=F32) + base_scr[0:1, :]
    rank1 = jnp.sum(sel1 * prior, axis=1, keepdims=True)
    rank2 = jnp.sum(sel2 * prior, axis=1, keepdims=True)
    out = jnp.where(lane == 0, i1.astype(F32), 0.0)
    out = jnp.where(lane == 1, i2.astype(F32), out)
    out = jnp.where(lane == 2, g1, out)
    out = jnp.where(lane == 3, g2, out)
    out = jnp.where(lane == 4, rank1, out)
    out = jnp.where(lane == 5, rank2, out)
    route_ref[...] = out
    total = base_scr[0:1, :] + jnp.sum(onehot, axis=0, keepdims=True)
    base_scr[...] = jnp.broadcast_to(total, base_scr.shape)
    cnt_ref[...] = jnp.broadcast_to(total, cnt_ref.shape)


def _router(x, w_pad, rows):
    d = x.shape[1]
    return pl.pallas_call(
        _router_kernel, grid=(rows // ROW_BLOCK,),
        in_specs=[pl.BlockSpec((ROW_BLOCK, d), lambda i: (i, 0)), pl.BlockSpec((d, LANES), lambda i: (0, 0))],
        out_specs=[pl.BlockSpec((ROW_BLOCK, LANES), lambda i: (i, 0)),
                   pl.BlockSpec((8, LANES), lambda i: (0, 0)),
                   pl.BlockSpec((ROW_BLOCK, d), lambda i: (i, 0))],
        out_shape=[jax.ShapeDtypeStruct((rows, LANES), F32), jax.ShapeDtypeStruct((8, LANES), F32),
                   jax.ShapeDtypeStruct((rows, d), BF16)],
        scratch_shapes=[pltpu.VMEM((8, LANES), F32)],
        compiler_params=_cparams(("arbitrary",)), name="router")(x, w_pad)


def _moe_kernel(be_ref, nu_ref, x_ref, wg_ref, wu_ref, wd_ref, o_ref, acc_ref):
    i, j = pl.program_id(0), pl.program_id(1)

    @pl.when(i < nu_ref[0])
    def _():
        @pl.when(j == 0)
        def _():
            acc_ref[...] = jnp.zeros_like(acc_ref)

        x = x_ref[...]
        gate = jnp.dot(x, wg_ref[...], preferred_element_type=F32)
        up = jnp.dot(x, wu_ref[...], preferred_element_type=F32)
        hid = (gate * jax.nn.sigmoid(gate) * up).astype(BF16)
        acc_ref[...] += jnp.dot(hid, wd_ref[...], preferred_element_type=F32)

        @pl.when(j == pl.num_programs(1) - 1)
        def _():
            o_ref[...] = acc_ref[...]


def _moe_ffn(xs, blk_e, n_used, wg, wu, wd):
    rows, d = xs.shape
    n_blk, n_ff = rows // MOE_BM, D_FF_EXPERT // MOE_TF

    def bi(i, nu):
        return jnp.minimum(i, nu[0] - 1)

    def fj(i, j, nu):
        return jnp.where(i < nu[0], j, n_ff - 1)

    grid_spec = pltpu.PrefetchScalarGridSpec(
        num_scalar_prefetch=2, grid=(n_blk, n_ff),
        in_specs=[pl.BlockSpec((MOE_BM, d), lambda i, j, be, nu: (bi(i, nu), 0)),
                  pl.BlockSpec((None, d, MOE_TF), lambda i, j, be, nu: (be[bi(i, nu)], 0, fj(i, j, nu))),
                  pl.BlockSpec((None, d, MOE_TF), lambda i, j, be, nu: (be[bi(i, nu)], 0, fj(i, j, nu))),
                  pl.BlockSpec((None, MOE_TF, d), lambda i, j, be, nu: (be[bi(i, nu)], fj(i, j, nu), 0))],
        out_specs=pl.BlockSpec((MOE_BM, d), lambda i, j, be, nu: (bi(i, nu), 0)),
        scratch_shapes=[pltpu.VMEM((MOE_BM, d), F32)])
    return pl.pallas_call(
        _moe_kernel, grid_spec=grid_spec, out_shape=jax.ShapeDtypeStruct((rows, d), F32),
        compiler_params=_cparams(("arbitrary", "arbitrary")), name="moe_ffn")(blk_e, n_used, xs, wg, wu, wd)


def _combine_kernel(x_ref, y0_ref, y1_ref, route_ref, g_ref, b_ref, o_ref):
    g0 = route_ref[:, 2:3]
    g1 = route_ref[:, 3:4]
    ffn = y0_ref[...] * g0 + y1_ref[...] * g1
    o_ref[...] = _ln(DEEPNORM_ALPHA * x_ref[...] + ffn, g_ref[...], b_ref[...])


def _combine(x, y0, y1, route, g, b, rows):
    d = x.shape[1]
    row = lambda w: pl.BlockSpec((ROW_BLOCK, w), lambda i: (i, 0))
    return pl.pallas_call(
        _combine_kernel, grid=(rows // ROW_BLOCK,),
        in_specs=[row(d), row(d), row(d), row(LANES),
                  pl.BlockSpec((1, d), lambda i: (0, 0)), pl.BlockSpec((1, d), lambda i: (0, 0))],
        out_specs=row(d), out_shape=jax.ShapeDtypeStruct((rows, d), F32),
        compiler_params=_cparams(("parallel",)), name="moe_combine")(x, y0, y1, route, g, b)


def _moe_layer(h, w_router, wg, wu, wd, g, b, rows):
    d = h.shape[1]
    w_pad = jnp.zeros((d, LANES), F32).at[:, :N_EXPERTS].set(w_router)
    route, cnt, hb = _router(h, w_pad, rows)
    idx = route[:, 0:2].astype(jnp.int32)
    rank = route[:, 4:6].astype(jnp.int32)
    counts = cnt[0, :N_EXPERTS].astype(jnp.int32)
    pcounts = (counts + MOE_BM - 1) // MOE_BM * MOE_BM
    pend = jnp.cumsum(pcounts)
    pstart = pend - pcounts
    dest = pstart[idx] + rank
    n_blk = -(-(2 * rows + N_EXPERTS * (MOE_BM - 1)) // MOE_BM)
    blk_e = jnp.minimum(jnp.searchsorted(pend, jnp.arange(n_blk, dtype=jnp.int32) * MOE_BM, side='right'),
                        N_EXPERTS - 1).astype(jnp.int32)
    n_used = (pend[-1:] // MOE_BM).astype(jnp.int32)
    tok = jnp.repeat(jnp.arange(rows, dtype=jnp.int32), 2)
    tok_buf = jnp.full((n_blk * MOE_BM,), rows, jnp.int32).at[dest.reshape(-1)].set(tok)
    xs = jnp.concatenate([hb, jnp.zeros((1, d), BF16)], axis=0)[tok_buf]
    yb = _moe_ffn(xs, blk_e, n_used, wg, wu, wd)
    return _combine(h, yb[dest[:, 0]], yb[dest[:, 1]], route, g, b, rows)


def kernel(x, meta, w_in_a, b_gate_a, norm_a, w_out_a, w_kv, w_q_b, w_o_b, w_gate_d, w_up_d, w_down_d,
           w_router, w_gate_e, w_up_e, w_down_e, ln_g, ln_b):
    bsz, seq, d = x.shape
    n_real = bsz * seq
    hk, hv = A_HEADS * A_QK_DIM, A_HEADS * A_V_DIM
    bw = B_HEADS * B_HEAD_DIM
    lnp = lambda l, i: (ln_g[l, i].reshape(1, d), ln_b[l, i].reshape(1, d))

    meta_blk = jnp.concatenate([jnp.zeros((META_PAD, d), x.dtype), meta.astype(x.dtype)], axis=0)
    h0 = jnp.concatenate([x.reshape(n_real, d), jnp.tile(meta_blk, (bsz, 1))], axis=0)
    rows = h0.shape[0]

    w_in = w_in_a[0]
    w_gates = jnp.zeros((d, LANES), F32).at[:, :2 * A_HEADS].set(w_in[:, 2 * hk + 2 * hv:])
    ws = [w_in[:, :hk], w_in[:, hk:2 * hk], w_in[:, 2 * hk:2 * hk + hv], w_in[:, 2 * hk + hv:2 * hk + 2 * hv], w_gates]
    q, k, v, o, gates = _proj(h0, [w.astype(BF16) for w in ws], [1.0, A_QK_DIM ** -0.5, 1.0, 1.0, 1.0],
                              [BF16, BF16, BF16, BF16, F32], rows)
    gates_t = gates[:, :2 * A_HEADS].T
    mixed = _mlstm(q, k, v, o, gates, gates_t, b_gate_a[0], norm_a[0], bsz, seq)
    h1 = _mm_res_ln(mixed, w_out_a[0].astype(BF16), h0, *lnp(0, 0))

    n_ff = D_FF_DENSE // FF_CHUNK
    wg3 = w_gate_d[0].astype(BF16).reshape(d, n_ff, FF_CHUNK).transpose(1, 0, 2)
    wu3 = w_up_d[0].astype(BF16).reshape(d, n_ff, FF_CHUNK).transpose(1, 0, 2)
    wd3 = w_down_d[0].astype(BF16).reshape(n_ff, FF_CHUNK, d)
    h2 = _ffn(h1, wg3, wu3, wd3, *lnp(0, 1))

    k_sh, v_sh = _proj(h2, [w_kv[:, :bw].astype(BF16), w_kv[:, bw:].astype(BF16)], [1.0, 1.0], [BF16, BF16], rows)
    (q_b,) = _proj(h2, [w_q_b[0].astype(BF16)], [B_HEAD_DIM ** -0.5], [BF16], n_real)
    att = _attention(q_b, k_sh, v_sh, bsz, seq)
    h3 = _mm_res_ln(att, w_o_b[0].astype(BF16), h2, *lnp(1, 0))

    h4 = _moe_layer(h3, w_router[0], w_gate_e[0].astype(BF16), w_up_e[0].astype(BF16), w_down_e[0].astype(BF16),
                    *lnp(1, 1), n_real)
    return h4.reshape(bsz, seq, d)
```
